```python
import math
import jax, jax.numpy as jnp
from jax import lax
import numpy as np

D_MODEL = 4096
BATCH = 2
SEQ = 4096
DEPTH = 4

ROPE_THETA = 500000.0
BLOCK = 128
A_HEAD_DIM = 128
A_PATTERNS = ((128, 1), (512, 4), (2048, 16))
A_GROUPS = len(A_PATTERNS)
A_HEADS_PER_GROUP = D_MODEL // 1024
A_HEADS = A_GROUPS * A_HEADS_PER_GROUP
A_WIDTH = A_HEADS * A_HEAD_DIM
A_OUT = A_HEADS_PER_GROUP * A_HEAD_DIM
B_HEAD_DIM = 64
B_Q_HEADS = D_MODEL // 128
B_KV_HEADS = B_Q_HEADS // 8
B_GQA = B_Q_HEADS // B_KV_HEADS
B_WINDOW = 128
B_Q_WIDTH = B_Q_HEADS * B_HEAD_DIM
B_KV_WIDTH = B_KV_HEADS * B_HEAD_DIM
IN_COLS = 3 * A_WIDTH + B_Q_WIDTH + 2 * B_KV_WIDTH + 2 * D_MODEL
SPLITS = tuple(int(s) for s in np.cumsum([A_WIDTH, A_WIDTH, A_WIDTH, B_Q_WIDTH, B_KV_WIDTH, B_KV_WIDTH]))
D_FF = 2 * D_MODEL
CONV_WIDTH = 3
DEEPNORM_ALPHA = (2.0 * DEPTH) ** 0.25
DEEPNORM_BETA = (8.0 * DEPTH) ** -0.25
LN_EPS = 1e-5

kernel_name = "hybrid_dilated_swa_sink_convffn_deepnorm"


def layer_norm(x, g, b):
    xf = x.astype(jnp.float32)
    mu = jnp.mean(xf, axis=-1, keepdims=True)
    var = jnp.mean(jnp.square(xf - mu), axis=-1, keepdims=True)
    y = (xf - mu) * lax.rsqrt(var + LN_EPS)
    return (y * g.astype(jnp.float32) + b.astype(jnp.float32)).astype(x.dtype)


def rope_tables(positions, rot_dim):
    inv_freq = ROPE_THETA ** (-(jnp.arange(0, rot_dim, 2, dtype=jnp.float32) / rot_dim))
    ang = positions.astype(jnp.float32)[..., None] * inv_freq
    return jnp.cos(ang)[:, :, None, :], jnp.sin(ang)[:, :, None, :]


def apply_partial_rope(t, cos, sin):
    half = cos.shape[-1]
    rot = 2 * half
    tf = t[..., :rot].astype(jnp.float32)
    x1, x2 = tf[..., :half], tf[..., half:]
    rotated = jnp.concatenate([x1 * cos - x2 * sin, x2 * cos + x1 * sin], axis=-1)
    return jnp.concatenate([rotated.astype(t.dtype), t[..., rot:]], axis=-1)


def dilate(t, r):
    b, s = t.shape[0], t.shape[1]
    rest = t.shape[2:]
    t = t.reshape((b, s // r, r) + rest)
    t = jnp.moveaxis(t, 2, 1)
    return t.reshape((b * r, s // r) + rest)


def undilate(t, r, b):
    l = t.shape[1]
    rest = t.shape[2:]
    t = t.reshape((b, r, l) + rest)
    t = jnp.moveaxis(t, 1, 2)
    return t.reshape((b, l * r) + rest)


def banded_attention(q, k, v, max_dist, sinks=None):
    n, l, hkv, g, d = q.shape
    lp = -(-l // BLOCK) * BLOCK
    pad = lp - l
    if pad:
        q = jnp.pad(q, ((0, 0), (0, pad), (0, 0), (0, 0), (0, 0)))
        k = jnp.pad(k, ((0, 0), (0, pad), (0, 0), (0, 0)))
        v = jnp.pad(v, ((0, 0), (0, pad), (0, 0), (0, 0)))
    nb = lp // BLOCK
    qb = q.reshape(n, nb, BLOCK, hkv, g, d)
    kb = k.reshape(n, nb, BLOCK, hkv, d)
    vb = v.reshape(n, nb, BLOCK, hkv, d)
    kc = jnp.concatenate([jnp.pad(kb, ((0, 0), (1, 0), (0, 0), (0, 0), (0, 0)))[:, :-1], kb], axis=2)
    vc = jnp.concatenate([jnp.pad(vb, ((0, 0), (1, 0), (0, 0), (0, 0), (0, 0)))[:, :-1], vb], axis=2)
    scale = 1.0 / math.sqrt(d)
    s = jnp.einsum('nbqhgd,nbkhd->nbhgqk', qb, kc).astype(jnp.float32) * scale
    qi = jnp.arange(BLOCK)[:, None] + BLOCK
    kj = jnp.arange(2 * BLOCK)[None, :]
    dist = qi - kj
    key_abs = (jnp.arange(nb) * BLOCK)[:, None] - BLOCK + jnp.arange(2 * BLOCK)[None, :]
    mask = ((dist >= 0) & (dist <= max_dist))[None] & (key_abs >= 0)[:, None, :]
    s = jnp.where(mask[None, :, None, None], s, -jnp.inf)
    m = jnp.max(s, axis=-1, keepdims=True)
    if sinks is not None:
        sink = sinks.astype(jnp.float32).reshape(hkv, g)[None, None, :, :, None, None]
        m = jnp.maximum(m, sink)
        p = jnp.exp(s - m)
        denom = jnp.sum(p, axis=-1, keepdims=True) + jnp.exp(sink - m)
    else:
        p = jnp.exp(s - m)
        denom = jnp.sum(p, axis=-1, keepdims=True)
    out = jnp.einsum('nbhgqk,nbkhd->nbqhgd', (p / denom).astype(v.dtype), vc)
    lse = (m + jnp.log(denom))[..., 0]
    out = out.reshape(n, lp, hkv, g, d)[:, :l]
    lse = jnp.transpose(lse, (0, 1, 4, 2, 3)).reshape(n, lp, hkv, g)[:, :l]
    return out, lse


def mixing_sublayer(x, rope_a, rope_b, w_in, w_proj_a, w_proj_b, w_out, sinks):
    b, s, _ = x.shape
    proj = jnp.einsum('bsd,de->bse', x, w_in)
    qa, ka, va, qb, kb, vb, gates = jnp.split(proj, SPLITS, axis=-1)
    qa = apply_partial_rope(qa.reshape(b, s, A_HEADS, A_HEAD_DIM), *rope_a)
    ka = apply_partial_rope(ka.reshape(b, s, A_HEADS, A_HEAD_DIM), *rope_a)
    va = va.reshape(b, s, A_HEADS, A_HEAD_DIM)
    outs, lses = [], []
    for gi, (window, dil) in enumerate(A_PATTERNS):
        hs = slice(gi * A_HEADS_PER_GROUP, (gi + 1) * A_HEADS_PER_GROUP)
        qg = dilate(qa[:, :, hs], dil)[:, :, :, None, :]
        kg = dilate(ka[:, :, hs], dil)
        vg = dilate(va[:, :, hs], dil)
        o, lse = banded_attention(qg, kg, vg, window // dil)
        outs.append(undilate(o[:, :, :, 0], dil, b))
        lses.append(undilate(lse[..., 0], dil, b))
    wts = jax.nn.softmax(jnp.stack(lses, axis=0), axis=0)
    ya = jnp.sum(wts[..., None] * jnp.stack(outs, axis=0).astype(jnp.float32), axis=0)
    ya = ya.astype(x.dtype).reshape(b, s, A_OUT)
    qb = apply_partial_rope(qb.reshape(b, s, B_Q_HEADS, B_HEAD_DIM), *rope_b)
    kb = apply_partial_rope(kb.reshape(b, s, B_KV_HEADS, B_HEAD_DIM), *rope_b)
    vb = vb.reshape(b, s, B_KV_HEADS, B_HEAD_DIM)
    yb, _ = banded_attention(qb.reshape(b, s, B_KV_HEADS, B_GQA, B_HEAD_DIM), kb, vb,
                             B_WINDOW - 1, sinks=sinks)
    yb = yb.reshape(b, s, B_Q_WIDTH)
    gate_a, gate_b = jnp.split(jax.nn.sigmoid(gates), 2, axis=-1)
    merged = (gate_a * jnp.einsum('bse,ed->bsd', ya, w_proj_a)
              + gate_b * jnp.einsum('bse,ed->bsd', yb, w_proj_b))
    return jnp.einsum('bsd,de->bse', merged, w_out)


def conv_ffn(x, w_up, conv_w, conv_b, w_down):
    u = jnp.einsum('bsd,df->bsf', x, w_up)
    up = jnp.pad(u, ((0, 0), (CONV_WIDTH - 1, 0), (0, 0)))
    s = x.shape[1]
    u = sum(conv_w[i] * up[:, i:i + s] for i in range(CONV_WIDTH)) + conv_b
    g, val = jnp.split(u, 2, axis=-1)
    return jnp.einsum('bsf,fd->bsd', jax.nn.silu(g) * val, w_down)


def setup_inputs(seed: int = 0) -> dict:
    key = jax.random.key(seed)
    ks = jax.random.split(key, 16)
    nrm = lambda k, shape, std: jax.random.normal(k, shape, jnp.float32) * std
    x = jax.random.normal(ks[0], (BATCH, SEQ, D_MODEL), jnp.float32)
    offset = jax.random.randint(ks[1], (BATCH, 1), 0, 1024, dtype=jnp.int32)
    positions = offset + jnp.arange(SEQ, dtype=jnp.int32)[None, :]
    return {
        "x": x,
        "positions": positions,
        "w_in": nrm(ks[2], (DEPTH, D_MODEL, IN_COLS), D_MODEL ** -0.5),
        "w_proj_a": nrm(ks[3], (DEPTH, A_OUT, D_MODEL), DEEPNORM_BETA * A_OUT ** -0.5),
        "w_proj_b": nrm(ks[4], (DEPTH, B_Q_WIDTH, D_MODEL), DEEPNORM_BETA * B_Q_WIDTH ** -0.5),
        "w_out": nrm(ks[5], (DEPTH, D_MODEL, D_MODEL), DEEPNORM_BETA * D_MODEL ** -0.5),
        "sinks": nrm(ks[6], (DEPTH, B_Q_HEADS), 0.5),
        "ln1_g": 1.0 + nrm(ks[7], (DEPTH, D_MODEL), 0.02),
        "ln1_b": nrm(ks[8], (DEPTH, D_MODEL), 0.02),
        "w_up": nrm(ks[9], (DEPTH, D_MODEL, 2 * D_FF), D_MODEL ** -0.5),
        "conv_w": nrm(ks[10], (DEPTH, CONV_WIDTH, 2 * D_FF), CONV_WIDTH ** -0.5),
        "conv_b": nrm(ks[11], (DEPTH, 2 * D_FF), 0.01),
        "w_down": nrm(ks[12], (DEPTH, D_FF, D_MODEL), DEEPNORM_BETA * D_FF ** -0.5),
        "ln2_g": 1.0 + nrm(ks[13], (DEPTH, D_MODEL), 0.02),
        "ln2_b": nrm(ks[14], (DEPTH, D_MODEL), 0.02),
    }


def reference(x, positions, w_in, w_proj_a, w_proj_b, w_out, sinks, ln1_g, ln1_b,
              w_up, conv_w, conv_b, w_down, ln2_g, ln2_b):
    rope_a = rope_tables(positions, A_HEAD_DIM // 4)
    rope_b = rope_tables(positions, B_HEAD_DIM // 4)
    for l in range(DEPTH):
        mix = mixing_sublayer(x, rope_a, rope_b, w_in[l], w_proj_a[l], w_proj_b[l], w_out[l], sinks[l])
        x = layer_norm(DEEPNORM_ALPHA * x + mix, ln1_g[l], ln1_b[l])
        f = conv_ffn(x, w_up[l], conv_w[l], conv_b[l], w_down[l])
        x = layer_norm(DEEPNORM_ALPHA * x + f, ln2_g[l], ln2_b[l])
    return x
```

```python
import functools
import math

import jax
import jax.numpy as jnp
from jax import lax
from jax.experimental import pallas as pl
from jax.experimental.pallas import tpu as pltpu

D_MODEL = 4096
DEPTH = 4
ROPE_THETA = 500000.0
BLOCK = 128
A_HEAD_DIM = 128
A_DILATIONS = (1, 4, 16)
A_GROUPS = len(A_DILATIONS)
A_HEADS_PER_GROUP = D_MODEL // 1024
A_HEADS = A_GROUPS * A_HEADS_PER_GROUP
A_WIDTH = A_HEADS * A_HEAD_DIM
A_OUT = A_HEADS_PER_GROUP * A_HEAD_DIM
A_MAX_DIST = 128
B_HEAD_DIM = 64
B_Q_HEADS = D_MODEL // 128
B_KV_HEADS = B_Q_HEADS // 8
B_GQA = B_Q_HEADS // B_KV_HEADS
B_MAX_DIST = 127
B_Q_WIDTH = B_Q_HEADS * B_HEAD_DIM
B_KV_WIDTH = B_KV_HEADS * B_HEAD_DIM
D_FF = 2 * D_MODEL
CONV_WIDTH = 3
DEEPNORM_ALPHA = (2.0 * DEPTH) ** 0.25
LN_EPS = 1e-5

OFF_QA = 0
OFF_VA = 2 * A_WIDTH
OFF_QB = 3 * A_WIDTH
OFF_KB = OFF_QB + B_Q_WIDTH
OFF_VB = OFF_KB + B_KV_WIDTH
OFF_GATES = OFF_VB + B_KV_WIDTH

LANES = 128
SUBLANES = 8
VMEM_CAPACITY = 64 * 1024 * 1024
BF16 = jnp.bfloat16
F32 = jnp.float32
NEG_INF = float("-inf")


def _vmem_limit(block_bytes, scratch_bytes):
    need = 2 * block_bytes + scratch_bytes + 8 * 1024 * 1024
    return int(min(need, VMEM_CAPACITY - 4 * 1024 * 1024))


def _nbytes(shape, dtype):
    return math.prod(shape) * jnp.dtype(dtype).itemsize


def _params(blocks, scratch=()):
    bb = sum(_nbytes(s, d) for s, d in blocks)
    sb = sum(_nbytes(s, d) for s, d in scratch)
    return pltpu.CompilerParams(
        dimension_semantics=("arbitrary",) * 2,
        vmem_limit_bytes=_vmem_limit(bb, sb))


def _cast_weight(w_ref, wbf_ref):
    k = w_ref.shape[0]
    chunk = 512
    def body(c, carry):
        rows = pl.ds(pl.multiple_of(c * chunk, chunk), chunk)
        wbf_ref[rows, :] = w_ref[rows, :].astype(BF16)
        return carry
    lax.fori_loop(0, k // chunk, body, 0)


def _rope(t, c, s1, s2, half):
    return (t * c + pltpu.roll(t, LANES - half, 1) * s1 + pltpu.roll(t, half, 1) * s2)


def _proj_body(*refs, mode, half):
    if mode in ("rope", "rope_expand"):
        x_ref, w_ref, c_ref, s1_ref, s2_ref, o_ref, wbf_ref = refs
    else:
        x_ref, w_ref, o_ref, wbf_ref = refs

    @pl.when(pl.program_id(1) == 0)
    def _():
        _cast_weight(w_ref, wbf_ref)

    acc = jnp.dot(x_ref[...], wbf_ref[...], preferred_element_type=F32)
    tn = acc.shape[1]
    if mode == "sigmoid":
        o_ref[...] = jax.nn.sigmoid(acc).astype(o_ref.dtype)
    elif mode == "plain":
        o_ref[...] = acc.astype(o_ref.dtype)
    else:
        lane = lax.broadcasted_iota(jnp.int32, (acc.shape[0], LANES), 1)
        low = lane < B_HEAD_DIM
        for h in range(tn // LANES):
            t = acc[:, LANES * h:LANES * (h + 1)]
            if mode in ("rope", "rope_expand"):
                t = _rope(t, c_ref[...], s1_ref[...], s2_ref[...], half)
            if mode in ("rope_expand", "expand"):
                sw = pltpu.roll(t, B_HEAD_DIM, 1)
                parts = (jnp.where(low, t, 0.0), jnp.where(low, 0.0, sw),
                         jnp.where(low, sw, 0.0), jnp.where(low, 0.0, t))
                for e, part in enumerate(parts):
                    col = LANES * (4 * h + e)
                    o_ref[:, col:col + LANES] = part.astype(o_ref.dtype)
            else:
                o_ref[:, LANES * h:LANES * (h + 1)] = t.astype(o_ref.dtype)


def _proj(x_bf, w_in, layer, *, off, width, tn, mode, out_dtype, tables=None, half=0, name):
    t_rows, d = x_bf.shape
    tm = 1024
    expand = 4 if mode in ("rope_expand", "expand") else 1
    grid = (width // tn, t_rows // tm)
    cb = off // tn
    in_specs = [
        pl.BlockSpec((tm, d), lambda j, i: (i, 0)),
        pl.BlockSpec((None, d, tn), lambda j, i: (layer, 0, cb + j)),
    ]
    args = [x_bf, w_in]
    blocks = [((tm, d), BF16), ((d, tn), F32), ((tm, tn * expand), out_dtype)]
    if tables is not None:
        for tab in tables:
            in_specs.append(pl.BlockSpec((tm, LANES), lambda j, i: (i, 0)))
            args.append(tab)
            blocks.append(((tm, LANES), F32))
    scratch = [((d, tn), BF16)]
    return pl.pallas_call(
        functools.partial(_proj_body, mode=mode, half=half),
        grid=grid,
        in_specs=in_specs,
        out_specs=pl.BlockSpec((tm, tn * expand), lambda j, i: (i, j)),
        out_shape=jax.ShapeDtypeStruct((t_rows, width * expand), out_dtype),
        scratch_shapes=[pltpu.VMEM(s, dt) for s, dt in scratch],
        compiler_params=_params(blocks, scratch),
        name=name,
    )(*args)


def _band_mask(i, blk, max_dist):
    row = lax.broadcasted_iota(jnp.int32, (BLOCK, 2 * BLOCK), 0)
    col = lax.broadcasted_iota(jnp.int32, (BLOCK, 2 * BLOCK), 1)
    dist = row + BLOCK - col
    mask = (dist >= 0) & (dist <= max_dist)
    if blk == 0:
        mask = mask & ((col >= BLOCK) | (i > 0))
    return mask


def _attn_a_kernel(q_ref, k_ref, kh_ref, v_ref, vh_ref, o_ref, lse_ref, *, nblk):
    i = pl.program_id(2)
    scale = 1.0 / math.sqrt(A_HEAD_DIM)
    for blk in range(nblk):
        mask = _band_mask(i, blk, A_MAX_DIST)
        rows = slice(BLOCK * blk, BLOCK * (blk + 1))
        prev = slice(BLOCK * (blk - 1), BLOCK * blk)
        for h in range(A_HEADS_PER_GROUP):
            cols = slice(A_HEAD_DIM * h, A_HEAD_DIM * (h + 1))
            q = q_ref[rows, cols]
            if blk == 0:
                kc = jnp.concatenate([kh_ref[:, cols], k_ref[rows, cols]], axis=0)
                vc = jnp.concatenate([vh_ref[:, cols], v_ref[rows, cols]], axis=0)
            else:
                kc = jnp.concatenate([k_ref[prev, cols], k_ref[rows, cols]], axis=0)
                vc = jnp.concatenate([v_ref[prev, cols], v_ref[rows, cols]], axis=0)
            s = lax.dot_general(q, kc, (((1,), (1,)), ((), ())),
                                preferred_element_type=F32) * scale
            s = jnp.where(mask, s, NEG_INF)
            m = jnp.max(s, axis=-1, keepdims=True)
            p = jnp.exp(s - m)
            denom = jnp.sum(p, axis=-1, keepdims=True)
            o = jnp.dot(p.astype(BF16), vc, preferred_element_type=F32)
            o_ref[rows, cols] = o * (1.0 / denom)
            lse_ref[rows, cols] = jnp.broadcast_to(m + jnp.log(denom), (BLOCK, A_HEAD_DIM))


def _attn_a(qk, v, batch, group, name):
    r = A_DILATIONS[group]
    t_rows = qk.shape[0]
    seq = t_rows // batch
    sub = seq // r
    tq = min(512, sub)
    nblk = tq // BLOCK
    qk_v = qk.reshape(batch, sub, r * 2 * A_WIDTH)
    v_v = v.reshape(batch, sub, r * A_WIDTH)
    qkb = 2 * A_WIDTH // A_OUT
    vb = A_WIDTH // A_OUT
    halo = lambda i: jnp.maximum(i * nblk - 1, 0)
    in_specs = [
        pl.BlockSpec((None, tq, A_OUT), lambda b, c, i: (b, i, c * qkb + group)),
        pl.BlockSpec((None, tq, A_OUT), lambda b, c, i: (b, i, c * qkb + A_GROUPS + group)),
        pl.BlockSpec((None, BLOCK, A_OUT), lambda b, c, i: (b, halo(i), c * qkb + A_GROUPS + group)),
        pl.BlockSpec((None, tq, A_OUT), lambda b, c, i: (b, i, c * vb + group)),
        pl.BlockSpec((None, BLOCK, A_OUT), lambda b, c, i: (b, halo(i), c * vb + group)),
    ]
    out_spec = pl.BlockSpec((None, tq, A_OUT), lambda b, c, i: (b, i, c))
    out_shape = jax.ShapeDtypeStruct((batch, sub, r * A_OUT), F32)
    out, lse = pl.pallas_call(
        functools.partial(_attn_a_kernel, nblk=nblk),
        grid=(batch, r, sub // tq),
        in_specs=in_specs,
        out_specs=(out_spec, out_spec),
        out_shape=(out_shape, out_shape),
        compiler_params=pltpu.CompilerParams(
            dimension_semantics=("arbitrary",) * 3,
            vmem_limit_bytes=48 * 1024 * 1024),
        name=name,
    )(qk_v, qk_v, qk_v, v_v, v_v)
    return out.reshape(t_rows, A_OUT), lse.reshape(t_rows, A_OUT)


def _combine_body(o0, o1, o2, l0, l1, l2, y_ref):
    m = jnp.maximum(jnp.maximum(l0[...], l1[...]), l2[...])
    e0 = jnp.exp(l0[...] - m)
    e1 = jnp.exp(l1[...] - m)
    e2 = jnp.exp(l2[...] - m)
    num = e0 * o0[...] + e1 * o1[...] + e2 * o2[...]
    y_ref[...] = (num * (1.0 / (e0 + e1 + e2))).astype(y_ref.dtype)


def _combine(outs, lses, name):
    t_rows = outs[0].shape[0]
    tr = 1024
    spec = pl.BlockSpec((tr, A_OUT), lambda i: (i, 0))
    return pl.pallas_call(
        _combine_body,
        grid=(t_rows // tr,),
        in_specs=[spec] * 6,
        out_specs=spec,
        out_shape=jax.ShapeDtypeStruct((t_rows, A_OUT), BF16),
        compiler_params=pltpu.CompilerParams(
            dimension_semantics=("arbitrary",),
            vmem_limit_bytes=40 * 1024 * 1024),
        name=name,
    )(*outs, *lses)


def _attn_b_kernel(sink_ref, q_ref, k_ref, kh_ref, v_ref, vh_ref, o_ref, *, nblk, layer):
    i = pl.program_id(1)
    scale = 1.0 / math.sqrt(B_HEAD_DIM)
    lane = lax.broadcasted_iota(jnp.int32, (BLOCK, LANES), 1)
    pairs = B_GQA // 2
    kvw = 4 * B_HEAD_DIM
    for blk in range(nblk):
        mask = _band_mask(i, blk, B_MAX_DIST)
        rows = slice(BLOCK * blk, BLOCK * (blk + 1))
        prev = slice(BLOCK * (blk - 1), BLOCK * blk)
        for hk in range(B_KV_HEADS):
            cols = slice(kvw * hk, kvw * (hk + 1))
            if blk == 0:
                kc = jnp.concatenate([kh_ref[:, cols], k_ref[rows, cols]], axis=0)
                vc = jnp.concatenate([vh_ref[:, cols], v_ref[rows, cols]], axis=0)
            else:
                kc = jnp.concatenate([k_ref[prev, cols], k_ref[rows, cols]], axis=0)
                vc = jnp.concatenate([v_ref[prev, cols], v_ref[rows, cols]], axis=0)
            kk = jnp.concatenate([kc[:, :LANES], kc[:, LANES:]], axis=0)
            vv = jnp.concatenate([vc[:, :LANES], vc[:, LANES:]], axis=0)
            for pr in range(pairs):
                qcols = slice(LANES * (pairs * hk + pr), LANES * (pairs * hk + pr + 1))
                q2 = q_ref[rows, qcols]
                s2 = lax.dot_general(q2, kk, (((1,), (1,)), ((), ())),
                                     preferred_element_type=F32) * scale
                ps, invs = [], []
                for e in range(2):
                    sink = sink_ref[layer, B_GQA * hk + 2 * pr + e]
                    s = jnp.where(mask, s2[:, 2 * BLOCK * e:2 * BLOCK * (e + 1)], NEG_INF)
                    m = jnp.maximum(jnp.max(s, axis=-1, keepdims=True), sink)
                    p = jnp.exp(s - m)
                    denom = jnp.sum(p, axis=-1, keepdims=True) + jnp.exp(sink - m)
                    ps.append(p.astype(BF16))
                    invs.append(1.0 / denom)
                o2 = jnp.dot(jnp.concatenate(ps, axis=1), vv, preferred_element_type=F32)
                o2 = o2 * jnp.where(lane < B_HEAD_DIM, invs[0], invs[1])
                o_ref[rows, qcols] = o2.astype(o_ref.dtype)


def _attn_b(q, k_exp, v_exp, sinks, batch, layer, name):
    t_rows = q.shape[0]
    seq = t_rows // batch
    tq = 512
    nblk = tq // BLOCK
    kvw = k_exp.shape[1]
    q_v = q.reshape(batch, seq, B_Q_WIDTH)
    k_v = k_exp.reshape(batch, seq, kvw)
    v_v = v_exp.reshape(batch, seq, kvw)
    halo = lambda i: jnp.maximum(i * nblk - 1, 0)
    own = pl.BlockSpec((None, tq, kvw), lambda b, i: (b, i, 0))
    hal = pl.BlockSpec((None, BLOCK, kvw), lambda b, i: (b, halo(i), 0))
    qspec = pl.BlockSpec((None, tq, B_Q_WIDTH), lambda b, i: (b, i, 0))
    out = pl.pallas_call(
        functools.partial(_attn_b_kernel, nblk=nblk, layer=layer),
        grid=(batch, seq // tq),
        in_specs=[pl.BlockSpec(memory_space=pltpu.SMEM), qspec, own, hal, own, hal],
        out_specs=qspec,
        out_shape=jax.ShapeDtypeStruct((batch, seq, B_Q_WIDTH), BF16),
        compiler_params=pltpu.CompilerParams(
            dimension_semantics=("arbitrary",) * 2,
            vmem_limit_bytes=48 * 1024 * 1024),
        name=name,
    )(sinks, q_v, k_v, k_v, v_v, v_v)
    return out.reshape(t_rows, B_Q_WIDTH)


def _merge_body(ya_ref, yb_ref, ga_ref, gb_ref, wa_ref, wb_ref, o_ref, wabf_ref, wbbf_ref):
    @pl.when(pl.program_id(1) == 0)
    def _():
        _cast_weight(wa_ref, wabf_ref)
        _cast_weight(wb_ref, wbbf_ref)

    pa = jnp.dot(ya_ref[...], wabf_ref[...], preferred_element_type=F32)
    pb = jnp.dot(yb_ref[...], wbbf_ref[...], preferred_element_type=F32)
    o_ref[...] = (ga_ref[...] * pa + gb_ref[...] * pb).astype(o_ref.dtype)


def _merge(ya, yb, gates, w_pa, w_pb, layer, name):
    t_rows = ya.shape[0]
    tm, tn = 1024, 512
    nb = D_MODEL // tn
    blocks = [((tm, A_OUT), BF16), ((tm, B_Q_WIDTH), BF16), ((tm, tn), F32), ((tm, tn), F32),
              ((A_OUT, tn), F32), ((B_Q_WIDTH, tn), F32), ((tm, tn), BF16)]
    scratch = [((A_OUT, tn), BF16), ((B_Q_WIDTH, tn), BF16)]
    return pl.pallas_call(
        _merge_body,
        grid=(nb, t_rows // tm),
        in_specs=[
            pl.BlockSpec((tm, A_OUT), lambda j, i: (i, 0)),
            pl.BlockSpec((tm, B_Q_WIDTH), lambda j, i: (i, 0)),
            pl.BlockSpec((tm, tn), lambda j, i: (i, j)),
            pl.BlockSpec((tm, tn), lambda j, i: (i, nb + j)),
            pl.BlockSpec((None, A_OUT, tn), lambda j, i: (layer, 0, j)),
            pl.BlockSpec((None, B_Q_WIDTH, tn), lambda j, i: (layer, 0, j)),
        ],
        out_specs=pl.BlockSpec((tm, tn), lambda j, i: (i, j)),
        out_shape=jax.ShapeDtypeStruct((t_rows, D_MODEL), BF16),
        scratch_shapes=[pltpu.VMEM(s, dt) for s, dt in scratch],
        compiler_params=_params(blocks, scratch),
        name=name,
    )(ya, yb, gates, gates, w_pa, w_pb)


def _resid_body(a_ref, x_ref, w_ref, o_ref, wbf_ref):
    @pl.when(pl.program_id(1) == 0)
    def _():
        _cast_weight(w_ref, wbf_ref)

    acc = jnp.dot(a_ref[...], wbf_ref[...], preferred_element_type=F32)
    o_ref[...] = DEEPNORM_ALPHA * x_ref[...] + acc


def _resid_matmul(a, x, w, layer, tm, tn, name):
    t_rows, k = a.shape
    blocks = [((tm, k), BF16), ((tm, tn), F32), ((k, tn), F32), ((tm, tn), F32)]
    scratch = [((k, tn), BF16)]
    return pl.pallas_call(
        _resid_body,
        grid=(D_MODEL // tn, t_rows // tm),
        in_specs=[
            pl.BlockSpec((tm, k), lambda j, i: (i, 0)),
            pl.BlockSpec((tm, tn), lambda j, i: (i, j)),
            pl.BlockSpec((None, k, tn), lambda j, i: (layer, 0, j)),
        ],
        out_specs=pl.BlockSpec((tm, tn), lambda j, i: (i, j)),
        out_shape=jax.ShapeDtypeStruct((t_rows, D_MODEL), F32),
        scratch_shapes=[pltpu.VMEM(s, dt) for s, dt in scratch],
        compiler_params=_params(blocks, scratch),
        name=name,
    )(a, x, w)


def _ln_body(h_ref, g_ref, b_ref, of_ref, ob_ref):
    h = h_ref[...]
    mu = jnp.mean(h, axis=-1, keepdims=True)
    d = h - mu
    var = jnp.mean(d * d, axis=-1, keepdims=True)
    y = d * lax.rsqrt(var + LN_EPS) * g_ref[...] + b_ref[...]
    of_ref[...] = y
    ob_ref[...] = y.astype(BF16)


def _layer_norm(h, g, b, layer, name):
    t_rows, d = h.shape
    tr = 256
    row = pl.BlockSpec((tr, d), lambda i: (i, 0))
    par = pl.BlockSpec((None, 1, d), lambda i: (layer, 0, 0))
    return pl.pallas_call(
        _ln_body,
        grid=(t_rows // tr,),
        in_specs=[row, par, par],
        out_specs=(row, row),
        out_shape=(jax.ShapeDtypeStruct((t_rows, d), F32), jax.ShapeDtypeStruct((t_rows, d), BF16)),
        compiler_params=pltpu.CompilerParams(
            dimension_semantics=("arbitrary",),
            vmem_limit_bytes=48 * 1024 * 1024),
        name=name,
    )(h, g, b)


def _ffn_up_body(x_ref, wg_ref, wv_ref, cwg_ref, cwv_ref, cbg_ref, cbv_ref, o_ref,
                 wbf_ref, ubuf_ref, *, tiles_per_seq):
    i = pl.program_id(1)
    tm = x_ref.shape[0]

    @pl.when(i == 0)
    def _():
        _cast_weight(wg_ref, wbf_ref.at[0])
        _cast_weight(wv_ref, wbf_ref.at[1])

    @pl.when(i % tiles_per_seq == 0)
    def _():
        ubuf_ref[:, 0:SUBLANES, :] = jnp.zeros((2, SUBLANES, ubuf_ref.shape[2]), F32)

    x = x_ref[...]

    def conv(e, cw_ref, cb_ref):
        u = jnp.dot(x, wbf_ref[e], preferred_element_type=F32)
        ubuf_ref[e, SUBLANES:SUBLANES + tm, :] = u
        c = (cw_ref[0:1, :] * ubuf_ref[e, SUBLANES - 2:SUBLANES - 2 + tm, :]
             + cw_ref[1:2, :] * ubuf_ref[e, SUBLANES - 1:SUBLANES - 1 + tm, :]
             + cw_ref[2:3, :] * u + cb_ref[...])
        ubuf_ref[e, 0:SUBLANES, :] = ubuf_ref[e, tm:tm + SUBLANES, :]
        return c

    g = conv(0, cwg_ref, cbg_ref)
    v = conv(1, cwv_ref, cbv_ref)
    o_ref[...] = (jax.nn.silu(g) * v).astype(o_ref.dtype)


def _ffn_up(x_bf, w_up, conv_w, conv_b, batch, layer, name):
    t_rows, d = x_bf.shape
    tm, tn = 1024, 256
    nb = D_FF // tn
    seq = t_rows // batch
    blocks = [((tm, d), BF16), ((d, tn), F32), ((d, tn), F32), ((tm, tn), BF16)]
    scratch = [((2, d, tn), BF16), ((2, tm + SUBLANES, tn), F32)]
    wspec = lambda o: pl.BlockSpec((None, d, tn), lambda j, i: (layer, 0, o + j))
    cwspec = lambda o: pl.BlockSpec((None, CONV_WIDTH, tn), lambda j, i: (layer, 0, o + j))
    cbspec = lambda o: pl.BlockSpec((None, 1, tn), lambda j, i: (layer, 0, o + j))
    return pl.pallas_call(
        functools.partial(_ffn_up_body, tiles_per_seq=seq // tm),
        grid=(nb, t_rows // tm),
        in_specs=[pl.BlockSpec((tm, d), lambda j, i: (i, 0)),
                  wspec(0), wspec(nb), cwspec(0), cwspec(nb), cbspec(0), cbspec(nb)],
        out_specs=pl.BlockSpec((tm, tn), lambda j, i: (i, j)),
        out_shape=jax.ShapeDtypeStruct((t_rows, D_FF), BF16),
        scratch_shapes=[pltpu.VMEM(s, dt) for s, dt in scratch],
        compiler_params=_params(blocks, scratch),
        name=name,
    )(x_bf, w_up, w_up, conv_w, conv_w, conv_b, conv_b)


def _rope_tables(positions, head_dim):
    rot = head_dim // 4
    half = rot // 2
    inv_freq = ROPE_THETA ** (-(jnp.arange(0, rot, 2, dtype=F32) / rot))
    ang = positions.astype(F32)[..., None] * inv_freq
    cos, sin = jnp.cos(ang), jnp.sin(ang)
    rest = head_dim - rot
    pad = lambda a, before, after: jnp.pad(a, ((0, 0), (0, 0), (before, after)))
    c = jnp.concatenate([cos, cos, jnp.ones(cos.shape[:2] + (rest,), F32)], axis=-1)
    s1 = pad(-sin, 0, half + rest)
    s2 = pad(sin, half, rest)
    reps = LANES // head_dim
    t_rows = positions.shape[0] * positions.shape[1]
    return tuple(jnp.tile(a, (1, 1, reps)).reshape(t_rows, LANES) for a in (c, s1, s2))


def kernel(x, positions, w_in, w_proj_a, w_proj_b, w_out, sinks, ln1_g, ln1_b,
           w_up, conv_w, conv_b, w_down, ln2_g, ln2_b):
    batch, seq, d = x.shape
    t_rows = batch * seq
    rope_a = _rope_tables(positions, A_HEAD_DIM)
    rope_b = _rope_tables(positions, B_HEAD_DIM)
    ln1_g, ln1_b, ln2_g, ln2_b, conv_b = (a.reshape(DEPTH, 1, a.shape[-1])
                                          for a in (ln1_g, ln1_b, ln2_g, ln2_b, conv_b))
    xf = x.reshape(t_rows, d)
    xb = xf.astype(BF16)
    for l in range(DEPTH):
        proj = functools.partial(_proj, xb, w_in, l)
        qk_a = proj(off=OFF_QA, width=2 * A_WIDTH, tn=512, mode="rope", out_dtype=BF16,
                    tables=rope_a, half=A_HEAD_DIM // 8, name=f"proj_qka_{l}")
        v_a = proj(off=OFF_VA, width=A_WIDTH, tn=512, mode="plain", out_dtype=BF16,
                   name=f"proj_va_{l}")
        q_b = proj(off=OFF_QB, width=B_Q_WIDTH, tn=512, mode="rope", out_dtype=BF16,
                   tables=rope_b, half=B_HEAD_DIM // 8, name=f"proj_qb_{l}")
        k_b = proj(off=OFF_KB, width=B_KV_WIDTH, tn=256, mode="rope_expand", out_dtype=BF16,
                   tables=rope_b, half=B_HEAD_DIM // 8, name=f"proj_kb_{l}")
        v_b = proj(off=OFF_VB, width=B_KV_WIDTH, tn=256, mode="expand", out_dtype=BF16,
                   name=f"proj_vb_{l}")
        gates = proj(off=OFF_GATES, width=2 * D_MODEL, tn=512, mode="sigmoid", out_dtype=F32,
                     name=f"proj_gates_{l}")
        outs, lses = zip(*(_attn_a(qk_a, v_a, batch, g, f"attn_a{g}_{l}") for g in range(A_GROUPS)))
        ya = _combine(outs, lses, f"combine_{l}")
        yb = _attn_b(q_b, k_b, v_b, sinks, batch, l, f"attn_b_{l}")
        merged = _merge(ya, yb, gates, w_proj_a, w_proj_b, l, f"merge_{l}")
        h = _resid_matmul(merged, xf, w_out, l, 1024, 512, f"out_proj_{l}")
        xf, xb = _layer_norm(h, ln1_g, ln1_b, l, f"ln1_{l}")
        a = _ffn_up(xb, w_up, conv_w, conv_b, batch, l, f"ffn_up_{l}")
        h = _resid_matmul(a, xf, w_down, l, 512, 256, f"ffn_down_{l}")
        xf, xb = _layer_norm(h, ln2_g, ln2_b, l, f"ln2_{l}")
    return xf.reshape(batch, seq, d)
```

```python
import functools
import math

import jax
import jax.numpy as jnp
from jax import lax
from jax.experimental import pallas as pl
from jax.experimental.pallas import tpu as pltpu

D_MODEL = 4096
DEPTH = 4
ROPE_THETA = 500000.0
BLOCK = 128
A_HEAD_DIM = 128
A_DILATIONS = (1, 4, 16)
A_GROUPS = len(A_DILATIONS)
A_HEADS_PER_GROUP = D_MODEL // 1024
A_HEADS = A_GROUPS * A_HEADS_PER_GROUP
A_WIDTH = A_HEADS * A_HEAD_DIM
A_OUT = A_HEADS_PER_GROUP * A_HEAD_DIM
A_MAX_DIST = 128
B_HEAD_DIM = 64
B_Q_HEADS = D_MODEL // 128
B_KV_HEADS = B_Q_HEADS // 8
B_GQA = B_Q_HEADS // B_KV_HEADS
B_MAX_DIST = 127
B_Q_WIDTH = B_Q_HEADS * B_HEAD_DIM
B_KV_WIDTH = B_KV_HEADS * B_HEAD_DIM
D_FF = 2 * D_MODEL
CONV_WIDTH = 3
DEEPNORM_ALPHA = (2.0 * DEPTH) ** 0.25
LN_EPS = 1e-5

OFF_QA = 0
OFF_KA = A_WIDTH
OFF_VA = 2 * A_WIDTH
OFF_QB = 3 * A_WIDTH
OFF_KB = OFF_QB + B_Q_WIDTH
OFF_VB = OFF_KB + B_KV_WIDTH
OFF_GATES = OFF_VB + B_KV_WIDTH

LANES = 128
SUBLANES = 8
VMEM_CAPACITY = 64 * 1024 * 1024
ROW_CHUNKS = 2
BF16 = jnp.bfloat16
F32 = jnp.float32
NEG_INF = float("-inf")


def _nbytes(shape, dtype):
    return math.prod(shape) * jnp.dtype(dtype).itemsize


def _params(blocks, scratch=()):
    need = (2 * sum(_nbytes(s, d) for s, d in blocks) + sum(_nbytes(s, d) for s, d in scratch)
            + 8 * 1024 * 1024)
    return pltpu.CompilerParams(
        dimension_semantics=("arbitrary",) * 2,
        vmem_limit_bytes=int(min(need, VMEM_CAPACITY - 4 * 1024 * 1024)))


def _cast_weight(w_ref, wbf_ref):
    k = w_ref.shape[0]
    chunk = 512
    def body(c, carry):
        rows = pl.ds(pl.multiple_of(c * chunk, chunk), chunk)
        wbf_ref[rows, :] = w_ref[rows, :].astype(BF16)
        return carry
    lax.fori_loop(0, k // chunk, body, 0)


def _row_chunks(tm):
    rows = tm // ROW_CHUNKS
    return [(c * rows, rows) for c in range(ROW_CHUNKS)]


def _rope(t, c, s1, s2, half):
    return (t * c + pltpu.roll(t, LANES - half, 1) * s1 + pltpu.roll(t, half, 1) * s2)


def _proj_body(*refs, rope_half, expand, sigmoid, dilation):
    refs = list(refs)
    x_ref, w_ref = refs[:2]
    tabs = refs[2:5] if rope_half else None
    o_ref = refs[5] if rope_half else refs[2]
    wbf_ref = refs[-2] if dilation > 1 else refs[-1]
    scr_ref = refs[-1] if dilation > 1 else None

    @pl.when(pl.program_id(1) == 0)
    def _():
        _cast_weight(w_ref, wbf_ref)

    tn = wbf_ref.shape[1]
    for r0, rows in _row_chunks(x_ref.shape[0]):
        rs = slice(r0, r0 + rows)
        acc = jnp.dot(x_ref[rs, :], wbf_ref[...], preferred_element_type=F32)
        if sigmoid:
            o_ref[rs, :] = jax.nn.sigmoid(acc).astype(o_ref.dtype)
            continue
        if not (rope_half or expand or dilation > 1):
            o_ref[rs, :] = acc.astype(o_ref.dtype)
            continue
        low = lax.broadcasted_iota(jnp.int32, (rows, LANES), 1) < B_HEAD_DIM
        for h in range(tn // LANES):
            cs = slice(LANES * h, LANES * (h + 1))
            t = acc[:, cs]
            if rope_half:
                t = _rope(t, tabs[0][rs, :], tabs[1][rs, :], tabs[2][rs, :], rope_half)
            if expand:
                sw = pltpu.roll(t, B_HEAD_DIM, 1)
                parts = (jnp.where(low, t, 0.0), jnp.where(low, 0.0, sw),
                         jnp.where(low, sw, 0.0), jnp.where(low, 0.0, t))
                for e, part in enumerate(parts):
                    col = LANES * (4 * h + e)
                    o_ref[rs, col:col + LANES] = part.astype(o_ref.dtype)
            elif dilation > 1:
                scr_ref[h, rs, :] = t
            else:
                o_ref[rs, cs] = t.astype(o_ref.dtype)
        if dilation > 1:
            n = rows // dilation
            for c in range(dilation):
                for h in range(tn // LANES):
                    o_ref[c, r0 // dilation:r0 // dilation + n, LANES * h:LANES * (h + 1)] = (
                        scr_ref[h, pl.ds(r0 + c, n, stride=dilation), :].astype(o_ref.dtype))


def _proj(x_bf, w_in, layer, *, col_block, n_tiles, tn, out_dtype, name, tables=None,
          rope_half=0, expand=False, sigmoid=False, dilation=1, batch=1):
    t_rows, d = x_bf.shape
    tm = 1024
    width = n_tiles * tn
    in_specs = [
        pl.BlockSpec((tm, d), lambda j, i: (i, 0)),
        pl.BlockSpec((None, d, tn), lambda j, i: (layer, 0, col_block(j))),
    ]
    args = [x_bf, w_in]
    blocks = [((tm, d), BF16), ((d, tn), F32)]
    if tables is not None:
        for tab in tables:
            in_specs.append(pl.BlockSpec((tm, LANES), lambda j, i: (i, 0)))
            args.append(tab)
            blocks.append(((tm, LANES), F32))
    scratch = [((d, tn), BF16)]
    if dilation > 1:
        seq = t_rows // batch
        tiles_per_seq = seq // tm
        out_spec = pl.BlockSpec((None, dilation, tm // dilation, tn),
                                lambda j, i: (i // tiles_per_seq, 0, i % tiles_per_seq, j))
        out_shape = jax.ShapeDtypeStruct((batch, dilation, seq // dilation, width), out_dtype)
        scratch.append(((tn // LANES, tm, LANES), F32))
        blocks.append(((tm, tn), out_dtype))
    else:
        ex = 4 if expand else 1
        out_spec = pl.BlockSpec((tm, tn * ex), lambda j, i: (i, j))
        out_shape = jax.ShapeDtypeStruct((t_rows, width * ex), out_dtype)
        blocks.append(((tm, tn * ex), out_dtype))
    return pl.pallas_call(
        functools.partial(_proj_body, rope_half=rope_half, expand=expand, sigmoid=sigmoid,
                          dilation=dilation),
        grid=(n_tiles, t_rows // tm),
        in_specs=in_specs,
        out_specs=out_spec,
        out_shape=out_shape,
        scratch_shapes=[pltpu.VMEM(s, dt) for s, dt in scratch],
        compiler_params=_params(blocks, scratch),
        name=name,
    )(*args)


def _band_mask(first, max_dist):
    row = lax.broadcasted_iota(jnp.int32, (BLOCK, 2 * BLOCK), 0)
    col = lax.broadcasted_iota(jnp.int32, (BLOCK, 2 * BLOCK), 1)
    dist = row + BLOCK - col
    mask = (dist >= 0) & (dist <= max_dist)
    if first is not None:
        mask = mask & ((col >= BLOCK) | jnp.logical_not(first))
    return mask


def _attn_a_kernel(q_ref, k_ref, kh_ref, v_ref, vh_ref, o_ref, lse_ref, *, dilation, nblk):
    first = pl.program_id(1) == 0
    scale = 1.0 / math.sqrt(A_HEAD_DIM)
    mask_first = _band_mask(first, A_MAX_DIST)
    mask_rest = _band_mask(None, A_MAX_DIST)
    for c in range(dilation):
        for blk in range(nblk):
            mask = mask_first if blk == 0 else mask_rest
            rows = slice(BLOCK * blk, BLOCK * (blk + 1))
            prev = slice(BLOCK * (blk - 1), BLOCK * blk)
            if dilation == 1:
                dst = rows
            else:
                dst = pl.ds(c + dilation * BLOCK * blk, BLOCK, stride=dilation)
            for h in range(A_HEADS_PER_GROUP):
                cols = slice(A_HEAD_DIM * h, A_HEAD_DIM * (h + 1))
                q = q_ref[c, rows, cols]
                if blk == 0:
                    kc = jnp.concatenate([kh_ref[c, :, cols], k_ref[c, rows, cols]], axis=0)
                    vc = jnp.concatenate([vh_ref[c, :, cols], v_ref[c, rows, cols]], axis=0)
                else:
                    kc = jnp.concatenate([k_ref[c, prev, cols], k_ref[c, rows, cols]], axis=0)
                    vc = jnp.concatenate([v_ref[c, prev, cols], v_ref[c, rows, cols]], axis=0)
                s = lax.dot_general(q, kc, (((1,), (1,)), ((), ())),
                                    preferred_element_type=F32) * scale
                s = jnp.where(mask, s, NEG_INF)
                m = jnp.max(s, axis=-1, keepdims=True)
                p = jnp.exp(s - m)
                denom = jnp.sum(p, axis=-1, keepdims=True)
                o = jnp.dot(p.astype(BF16), vc, preferred_element_type=F32)
                o_ref[h, dst, :] = o * (1.0 / denom)
                lse_ref[h, dst, :] = jnp.broadcast_to(m + jnp.log(denom), (BLOCK, A_HEAD_DIM))


def _attn_a(qk, v, group, name):
    batch, r, sub, _ = qk.shape
    assert r == A_DILATIONS[group]
    nblk = 4 if r < 16 else 1
    tq = nblk * BLOCK
    halo = lambda i: jnp.maximum(i * nblk - 1, 0)
    in_specs = [
        pl.BlockSpec((None, r, tq, A_OUT), lambda b, i: (b, 0, i, 0)),
        pl.BlockSpec((None, r, tq, A_OUT), lambda b, i: (b, 0, i, 1)),
        pl.BlockSpec((None, r, BLOCK, A_OUT), lambda b, i: (b, 0, halo(i), 1)),
        pl.BlockSpec((None, r, tq, A_OUT), lambda b, i: (b, 0, i, 0)),
        pl.BlockSpec((None, r, BLOCK, A_OUT), lambda b, i: (b, 0, halo(i), 0)),
    ]
    out_spec = pl.BlockSpec((None, A_HEADS_PER_GROUP, r * tq, A_HEAD_DIM), lambda b, i: (b, 0, i, 0))
    out_shape = jax.ShapeDtypeStruct((batch, A_HEADS_PER_GROUP, r * sub, A_HEAD_DIM), F32)
    return pl.pallas_call(
        functools.partial(_attn_a_kernel, dilation=r, nblk=nblk),
        grid=(batch, sub // tq),
        in_specs=in_specs,
        out_specs=(out_spec, out_spec),
        out_shape=(out_shape, out_shape),
        compiler_params=pltpu.CompilerParams(
            dimension_semantics=("arbitrary",) * 2,
            vmem_limit_bytes=48 * 1024 * 1024),
        name=name,
    )(qk, qk, qk, v, v)


def _combine_body(o0, o1, o2, l0, l1, l2, y_ref):
    for h in range(A_HEADS_PER_GROUP):
        m = jnp.maximum(jnp.maximum(l0[h], l1[h]), l2[h])
        e0 = jnp.exp(l0[h] - m)
        e1 = jnp.exp(l1[h] - m)
        e2 = jnp.exp(l2[h] - m)
        num = e0 * o0[h] + e1 * o1[h] + e2 * o2[h]
        y_ref[:, A_HEAD_DIM * h:A_HEAD_DIM * (h + 1)] = (
            num * (1.0 / (e0 + e1 + e2))).astype(y_ref.dtype)


def _combine(outs, lses, name):
    batch, heads, seq, hd = outs[0].shape
    tr = 1024
    tiles = seq // tr
    spec = pl.BlockSpec((None, heads, tr, hd), lambda b, i: (b, 0, i, 0))
    return pl.pallas_call(
        _combine_body,
        grid=(batch, tiles),
        in_specs=[spec] * 6,
        out_specs=pl.BlockSpec((tr, A_OUT), lambda b, i: (b * tiles + i, 0)),
        out_shape=jax.ShapeDtypeStruct((batch * seq, A_OUT), BF16),
        compiler_params=pltpu.CompilerParams(
            dimension_semantics=("arbitrary",) * 2,
            vmem_limit_bytes=40 * 1024 * 1024),
        name=name,
    )(*outs, *lses)


def _attn_b_kernel(sink_ref, q_ref, k_ref, kh_ref, v_ref, vh_ref, o_ref, *, nblk, layer):
    first = pl.program_id(1) == 0
    scale = 1.0 / math.sqrt(B_HEAD_DIM)
    lane = lax.broadcasted_iota(jnp.int32, (BLOCK, LANES), 1)
    pairs = B_GQA // 2
    kvw = 4 * B_HEAD_DIM
    mask_first = _band_mask(first, B_MAX_DIST)
    mask_rest = _band_mask(None, B_MAX_DIST)
    for blk in range(nblk):
        mask = mask_first if blk == 0 else mask_rest
        rows = slice(BLOCK * blk, BLOCK * (blk + 1))
        prev = slice(BLOCK * (blk - 1), BLOCK * blk)
        for hk in range(B_KV_HEADS):
            cols = slice(kvw * hk, kvw * (hk + 1))
            if blk == 0:
                kc = jnp.concatenate([kh_ref[:, cols], k_ref[rows, cols]], axis=0)
                vc = jnp.concatenate([vh_ref[:, cols], v_ref[rows, cols]], axis=0)
            else:
                kc = jnp.concatenate([k_ref[prev, cols], k_ref[rows, cols]], axis=0)
                vc = jnp.concatenate([v_ref[prev, cols], v_ref[rows, cols]], axis=0)
            kk = jnp.concatenate([kc[:, :LANES], kc[:, LANES:]], axis=0)
            vv = jnp.concatenate([vc[:, :LANES], vc[:, LANES:]], axis=0)
            for pr in range(pairs):
                qcols = slice(LANES * (pairs * hk + pr), LANES * (pairs * hk + pr + 1))
                q2 = q_ref[rows, qcols]
                s2 = lax.dot_general(q2, kk, (((1,), (1,)), ((), ())),
                                     preferred_element_type=F32) * scale
                ps, invs = [], []
                for e in range(2):
                    sink = sink_ref[layer, B_GQA * hk + 2 * pr + e]
                    s = jnp.where(mask, s2[:, 2 * BLOCK * e:2 * BLOCK * (e + 1)], NEG_INF)
                    m = jnp.maximum(jnp.max(s, axis=-1, keepdims=True), sink)
                    p = jnp.exp(s - m)
                    denom = jnp.sum(p, axis=-1, keepdims=True) + jnp.exp(sink - m)
                    ps.append(p.astype(BF16))
                    invs.append(1.0 / denom)
                o2 = jnp.dot(jnp.concatenate(ps, axis=1), vv, preferred_element_type=F32)
                o2 = o2 * jnp.where(lane < B_HEAD_DIM, invs[0], invs[1])
                o_ref[rows, qcols] = o2.astype(o_ref.dtype)


def _attn_b(q, k_exp, v_exp, sinks, batch, layer, name):
    t_rows = q.shape[0]
    seq = t_rows // batch
    tq = 512
    nblk = tq // BLOCK
    kvw = k_exp.shape[1]
    q_v = q.reshape(batch, seq, B_Q_WIDTH)
    k_v = k_exp.reshape(batch, seq, kvw)
    v_v = v_exp.reshape(batch, seq, kvw)
    halo = lambda i: jnp.maximum(i * nblk - 1, 0)
    own = pl.BlockSpec((None, tq, kvw), lambda b, i: (b, i, 0))
    hal = pl.BlockSpec((None, BLOCK, kvw), lambda b, i: (b, halo(i), 0))
    qspec = pl.BlockSpec((None, tq, B_Q_WIDTH), lambda b, i: (b, i, 0))
    out = pl.pallas_call(
        functools.partial(_attn_b_kernel, nblk=nblk, layer=layer),
        grid=(batch, seq // tq),
        in_specs=[pl.BlockSpec(memory_space=pltpu.SMEM), qspec, own, hal, own, hal],
        out_specs=qspec,
        out_shape=jax.ShapeDtypeStruct((batch, seq, B_Q_WIDTH), BF16),
        compiler_params=pltpu.CompilerParams(
            dimension_semantics=("arbitrary",) * 2,
            vmem_limit_bytes=48 * 1024 * 1024),
        name=name,
    )(sinks, q_v, k_v, k_v, v_v, v_v)
    return out.reshape(t_rows, B_Q_WIDTH)


def _merge_body(ya_ref, yb_ref, ga_ref, gb_ref, wa_ref, wb_ref, o_ref, wabf_ref, wbbf_ref):
    @pl.when(pl.program_id(1) == 0)
    def _():
        _cast_weight(wa_ref, wabf_ref)
        _cast_weight(wb_ref, wbbf_ref)

    for r0, rows in _row_chunks(ya_ref.shape[0]):
        rs = slice(r0, r0 + rows)
        pa = jnp.dot(ya_ref[rs, :], wabf_ref[...], preferred_element_type=F32)
        pb = jnp.dot(yb_ref[rs, :], wbbf_ref[...], preferred_element_type=F32)
        o_ref[rs, :] = (ga_ref[rs, :].astype(F32) * pa
                        + gb_ref[rs, :].astype(F32) * pb).astype(o_ref.dtype)


def _merge(ya, yb, gates, w_pa, w_pb, layer, name):
    t_rows = ya.shape[0]
    tm, tn = 512, 1024
    nb = D_MODEL // tn
    gdt = gates.dtype
    blocks = [((tm, A_OUT), BF16), ((tm, B_Q_WIDTH), BF16), ((tm, tn), gdt), ((tm, tn), gdt),
              ((A_OUT, tn), F32), ((B_Q_WIDTH, tn), F32), ((tm, tn), BF16)]
    scratch = [((A_OUT, tn), BF16), ((B_Q_WIDTH, tn), BF16)]
    return pl.pallas_call(
        _merge_body,
        grid=(nb, t_rows // tm),
        in_specs=[
            pl.BlockSpec((tm, A_OUT), lambda j, i: (i, 0)),
            pl.BlockSpec((tm, B_Q_WIDTH), lambda j, i: (i, 0)),
            pl.BlockSpec((tm, tn), lambda j, i: (i, j)),
            pl.BlockSpec((tm, tn), lambda j, i: (i, nb + j)),
            pl.BlockSpec((None, A_OUT, tn), lambda j, i: (layer, 0, j)),
            pl.BlockSpec((None, B_Q_WIDTH, tn), lambda j, i: (layer, 0, j)),
        ],
        out_specs=pl.BlockSpec((tm, tn), lambda j, i: (i, j)),
        out_shape=jax.ShapeDtypeStruct((t_rows, D_MODEL), BF16),
        scratch_shapes=[pltpu.VMEM(s, dt) for s, dt in scratch],
        compiler_params=_params(blocks, scratch),
        name=name,
    )(ya, yb, gates, gates, w_pa, w_pb)


def _resid_body(a_ref, x_ref, w_ref, o_ref, wbf_ref):
    @pl.when(pl.program_id(1) == 0)
    def _():
        _cast_weight(w_ref, wbf_ref)

    acc = jnp.dot(a_ref[...], wbf_ref[...], preferred_element_type=F32)
    o_ref[...] = DEEPNORM_ALPHA * x_ref[...] + acc


def _resid_matmul(a, x, w, layer, tm, tn, weight_buffers, name):
    t_rows, k = a.shape
    w_mode = {} if weight_buffers == 2 else {"pipeline_mode": pl.Buffered(weight_buffers)}
    blocks = [((tm, k), BF16), ((tm, tn), F32), ((tm, tn), F32)]
    scratch = [((k, tn), BF16)] + [((k, tn), F32)] * weight_buffers
    return pl.pallas_call(
        _resid_body,
        grid=(D_MODEL // tn, t_rows // tm),
        in_specs=[
            pl.BlockSpec((tm, k), lambda j, i: (i, 0)),
            pl.BlockSpec((tm, tn), lambda j, i: (i, j)),
            pl.BlockSpec((None, k, tn), lambda j, i: (layer, 0, j), **w_mode),
        ],
        out_specs=pl.BlockSpec((tm, tn), lambda j, i: (i, j)),
        out_shape=jax.ShapeDtypeStruct((t_rows, D_MODEL), F32),
        scratch_shapes=[pltpu.VMEM((k, tn), BF16)],
        compiler_params=_params(blocks, scratch),
        name=name,
    )(a, x, w)


def _ln_body(h_ref, g_ref, b_ref, of_ref, ob_ref):
    h = h_ref[...]
    mu = jnp.mean(h, axis=-1, keepdims=True)
    d = h - mu
    var = jnp.mean(d * d, axis=-1, keepdims=True)
    y = d * lax.rsqrt(var + LN_EPS) * g_ref[...] + b_ref[...]
    of_ref[...] = y
    ob_ref[...] = y.astype(BF16)


def _layer_norm(h, g, b, layer, name):
    t_rows, d = h.shape
    tr = 256
    row = pl.BlockSpec((tr, d), lambda i: (i, 0))
    par = pl.BlockSpec((None, 1, d), lambda i: (layer, 0, 0))
    return pl.pallas_call(
        _ln_body,
        grid=(t_rows // tr,),
        in_specs=[row, par, par],
        out_specs=(row, row),
        out_shape=(jax.ShapeDtypeStruct((t_rows, d), F32), jax.ShapeDtypeStruct((t_rows, d), BF16)),
        compiler_params=pltpu.CompilerParams(
            dimension_semantics=("arbitrary",),
            vmem_limit_bytes=48 * 1024 * 1024),
        name=name,
    )(h, g, b)


def _ffn_up_body(x_ref, wg_ref, wv_ref, cwg_ref, cwv_ref, cbg_ref, cbv_ref, o_ref,
                 wbf_ref, ubuf_ref, *, tiles_per_seq):
    i = pl.program_id(1)
    tm = x_ref.shape[0]

    @pl.when(i == 0)
    def _():
        _cast_weight(wg_ref, wbf_ref.at[0])
        _cast_weight(wv_ref, wbf_ref.at[1])

    @pl.when(i % tiles_per_seq == 0)
    def _():
        ubuf_ref[:, 0:SUBLANES, :] = jnp.zeros((2, SUBLANES, ubuf_ref.shape[2]), F32)

    def conv(e, x, r0, rows, cw_ref, cb_ref):
        u = jnp.dot(x, wbf_ref[e], preferred_element_type=F32)
        base = SUBLANES + r0
        ubuf_ref[e, base:base + rows, :] = u
        return (cw_ref[0:1, :] * ubuf_ref[e, base - 2:base - 2 + rows, :]
                + cw_ref[1:2, :] * ubuf_ref[e, base - 1:base - 1 + rows, :]
                + cw_ref[2:3, :] * u + cb_ref[...])

    for r0, rows in _row_chunks(tm):
        x = x_ref[r0:r0 + rows, :]
        g = conv(0, x, r0, rows, cwg_ref, cbg_ref)
        v = conv(1, x, r0, rows, cwv_ref, cbv_ref)
        o_ref[r0:r0 + rows, :] = (jax.nn.silu(g) * v).astype(o_ref.dtype)
    ubuf_ref[:, 0:SUBLANES, :] = ubuf_ref[:, tm:tm + SUBLANES, :]


def _ffn_up(x_bf, w_up, conv_w, conv_b, batch, layer, name):
    t_rows, d = x_bf.shape
    tm, tn = 1024, 256
    nb = D_FF // tn
    seq = t_rows // batch
    blocks = [((tm, d), BF16), ((d, tn), F32), ((d, tn), F32), ((tm, tn), BF16)]
    scratch = [((2, d, tn), BF16), ((2, tm + SUBLANES, tn), F32)]
    wspec = lambda o: pl.BlockSpec((None, d, tn), lambda j, i: (layer, 0, o + j))
    cwspec = lambda o: pl.BlockSpec((None, CONV_WIDTH, tn), lambda j, i: (layer, 0, o + j))
    cbspec = lambda o: pl.BlockSpec((None, 1, tn), lambda j, i: (layer, 0, o + j))
    return pl.pallas_call(
        functools.partial(_ffn_up_body, tiles_per_seq=seq // tm),
        grid=(nb, t_rows // tm),
        in_specs=[pl.BlockSpec((tm, d), lambda j, i: (i, 0)),
                  wspec(0), wspec(nb), cwspec(0), cwspec(nb), cbspec(0), cbspec(nb)],
        out_specs=pl.BlockSpec((tm, tn), lambda j, i: (i, j)),
        out_shape=jax.ShapeDtypeStruct((t_rows, D_FF), BF16),
        scratch_shapes=[pltpu.VMEM(s, dt) for s, dt in scratch],
        compiler_params=_params(blocks, scratch),
        name=name,
    )(x_bf, w_up, w_up, conv_w, conv_w, conv_b, conv_b)


def _rope_tables(positions, head_dim):
    rot = head_dim // 4
    half = rot // 2
    inv_freq = ROPE_THETA ** (-(jnp.arange(0, rot, 2, dtype=F32) / rot))
    ang = positions.astype(F32)[..., None] * inv_freq
    cos, sin = jnp.cos(ang), jnp.sin(ang)
    rest = head_dim - rot
    pad = lambda a, before, after: jnp.pad(a, ((0, 0), (0, 0), (before, after)))
    c = jnp.concatenate([cos, cos, jnp.ones(cos.shape[:2] + (rest,), F32)], axis=-1)
    s1 = pad(-sin, 0, half + rest)
    s2 = pad(sin, half, rest)
    reps = LANES // head_dim
    t_rows = positions.shape[0] * positions.shape[1]
    return tuple(jnp.tile(a, (1, 1, reps)).reshape(t_rows, LANES) for a in (c, s1, s2))


def kernel(x, positions, w_in, w_proj_a, w_proj_b, w_out, sinks, ln1_g, ln1_b,
           w_up, conv_w, conv_b, w_down, ln2_g, ln2_b):
    batch, seq, d = x.shape
    t_rows = batch * seq
    rope_a = _rope_tables(positions, A_HEAD_DIM)
    rope_b = _rope_tables(positions, B_HEAD_DIM)
    ln1_g, ln1_b, ln2_g, ln2_b, conv_b = (a.reshape(DEPTH, 1, a.shape[-1])
                                          for a in (ln1_g, ln1_b, ln2_g, ln2_b, conv_b))
    xf = x.reshape(t_rows, d)
    xb = xf.astype(BF16)
    for l in range(DEPTH):
        proj = functools.partial(_proj, xb, w_in, l, out_dtype=BF16)
        outs, lses = [], []
        for g, r in enumerate(A_DILATIONS):
            qk_a = proj(col_block=lambda j, g=g: (OFF_KA // A_OUT) * j + g, n_tiles=2, tn=A_OUT,
                        tables=rope_a, rope_half=A_HEAD_DIM // 8, dilation=r, batch=batch,
                        name=f"proj_qka{g}_{l}")
            v_a = proj(col_block=lambda j, g=g: OFF_VA // A_OUT + g, n_tiles=1, tn=A_OUT,
                       dilation=r, batch=batch, name=f"proj_va{g}_{l}")
            if r == 1:
                qk_a = qk_a.reshape(batch, 1, seq, 2 * A_OUT)
                v_a = v_a.reshape(batch, 1, seq, A_OUT)
            o, lse = _attn_a(qk_a, v_a, g, f"attn_a{g}_{l}")
            outs.append(o)
            lses.append(lse)
        ya = _combine(outs, lses, f"combine_{l}")
        q_b = proj(col_block=lambda j: OFF_QB // 512 + j, n_tiles=B_Q_WIDTH // 512, tn=512,
                   tables=rope_b, rope_half=B_HEAD_DIM // 8, name=f"proj_qb_{l}")
        k_b = proj(col_block=lambda j: OFF_KB // B_KV_WIDTH, n_tiles=1, tn=B_KV_WIDTH,
                   tables=rope_b, rope_half=B_HEAD_DIM // 8, expand=True, name=f"proj_kb_{l}")
        v_b = proj(col_block=lambda j: OFF_VB // B_KV_WIDTH, n_tiles=1, tn=B_KV_WIDTH,
                   expand=True, name=f"proj_vb_{l}")
        gates = proj(col_block=lambda j: OFF_GATES // 512 + j, n_tiles=2 * D_MODEL // 512, tn=512,
                     sigmoid=True, name=f"proj_gates_{l}")
        yb = _attn_b(q_b, k_b, v_b, sinks, batch, l, f"attn_b_{l}")
        merged = _merge(ya, yb, gates, w_proj_a, w_proj_b, l, f"merge_{l}")
        h = _resid_matmul(merged, xf, w_out, l, 1024, 512, 2, f"out_proj_{l}")
        xf, xb = _layer_norm(h, ln1_g, ln1_b, l, f"ln1_{l}")
        a = _ffn_up(xb, w_up, conv_w, conv_b, batch, l, f"ffn_up_{l}")
        h = _resid_matmul(a, xf, w_down, l, 512, 512, 1, f"ffn_down_{l}")
        xf, xb = _layer_norm(h, ln2_g, ln2_b, l, f"ln2_{l}")
    return xf.reshape(batch, seq, d)
```
